```python
import jax, jax.numpy as jnp
from jax import lax
import numpy as np

D_MODEL = 1024
BATCH = 8
SEQ = 2048
DEPTH = 1

ATT_HEADS = 8
ATT_KV_HEADS = 2
HEAD_DIM = 64
ATT_WIDTH = ATT_HEADS * HEAD_DIM
KV_WIDTH = ATT_KV_HEADS * HEAD_DIM
IDX_HEADS = 8
IDX_DIM = 64
TOPK_MAX = 256
Q_BLOCK = 128
ROPE_THETA = 500000.0
ROPE_FRACTION = 4
HG_HEADS = 4
HG_KDIM = 128
HG_VDIM = 128
HG_FWIDTH = HG_HEADS * HG_KDIM
HG_WIDTH = HG_HEADS * HG_VDIM
CHUNK = 64
MIX_WIDTH = ATT_WIDTH + HG_WIDTH
D_FF = ((-(-8 * D_MODEL // 3) + 255) // 256) * 256
NORM_EPS = 1e-6
IN_SPLITS = (ATT_WIDTH, KV_WIDTH, KV_WIDTH, IDX_HEADS * IDX_DIM, IDX_DIM, IDX_HEADS,
             HG_FWIDTH, HG_FWIDTH, HG_WIDTH, HG_WIDTH)
IN_WIDTH = sum(IN_SPLITS)

kernel_name = "hymba_dsa_hgrn2_sandwich_layer"


def rms_norm(x, gain):
    x32 = x.astype(jnp.float32)
    y = x32 * lax.rsqrt(jnp.mean(x32 * x32, axis=-1, keepdims=True) + NORM_EPS)
    return (y * gain.astype(jnp.float32)).astype(x.dtype)


def partial_rope(x, pos):
    d = x.shape[-1]
    rd = d // ROPE_FRACTION
    half = rd // 2
    inv = ROPE_THETA ** (-jnp.arange(half, dtype=jnp.float32) / half)
    ang = pos.astype(jnp.float32)[:, None] * inv[None, :]
    ang = ang.reshape((ang.shape[0],) + (1,) * (x.ndim - 3) + (half,))
    cos = jnp.cos(ang).astype(x.dtype)
    sin = jnp.sin(ang).astype(x.dtype)
    x1 = x[..., :half]
    x2 = x[..., half:rd]
    return jnp.concatenate([x1 * cos - x2 * sin, x2 * cos + x1 * sin, x[..., rd:]], axis=-1)


def split_columns(p):
    pts = []
    acc = 0
    for s in IN_SPLITS[:-1]:
        acc += s
        pts.append(acc)
    return jnp.split(p, pts, axis=-1)


def dsa_attention(q, k, v, qi, ki, wi):
    B, T = q.shape[0], q.shape[1]
    S = k.shape[1]
    topk = min(TOPK_MAX, S // 4)
    nb = T // Q_BLOCK
    key_pos = jnp.arange(S)
    group = ATT_HEADS // ATT_KV_HEADS

    def to_blocks(a):
        return jnp.swapaxes(a.reshape((B, nb, Q_BLOCK) + a.shape[2:]), 0, 1)

    qpos = jnp.arange(T).reshape(nb, Q_BLOCK)

    def block(args):
        qb, qib, wib, tpos = args
        rel = jax.nn.relu(jnp.einsum('bthd,bsd->btsh', qib, ki))
        iscore = jnp.einsum('btsh,bth->bts', rel, wib).astype(jnp.float32)
        causal = key_pos[None, :] <= tpos[:, None]
        iscore = jnp.where(causal[None], iscore, -jnp.inf)
        _, idx = lax.top_k(iscore, topk)
        valid = idx <= tpos[None, :, None]
        ksel = jax.vmap(lambda kk, ii: kk[ii])(k, idx)
        vsel = jax.vmap(lambda vv, ii: vv[ii])(v, idx)
        qg = qb.reshape(B, Q_BLOCK, ATT_KV_HEADS, group, HEAD_DIM)
        s = jnp.einsum('btgrd,btkgd->btgrk', qg, ksel).astype(jnp.float32) * (HEAD_DIM ** -0.5)
        s = jnp.where(valid[:, :, None, None, :], s, -jnp.inf)
        p = jax.nn.softmax(s, axis=-1).astype(v.dtype)
        o = jnp.einsum('btgrk,btkgd->btgrd', p, vsel)
        return o.reshape(B, Q_BLOCK, ATT_WIDTH)

    out = lax.map(block, (to_blocks(q), to_blocks(qi), to_blocks(wi), qpos))
    return jnp.swapaxes(out, 0, 1).reshape(B, T, ATT_WIDTH)


def hgrn2_chunkwise(q, k, v, logf):
    B, T, H, K = q.shape
    V = v.shape[-1]
    nc = T // CHUNK

    def chunks(a):
        return jnp.moveaxis(a.reshape(B, nc, CHUNK, H, a.shape[-1]), 1, 0)

    tri = jnp.tril(jnp.ones((CHUNK, CHUNK), dtype=bool))

    def step(state, inp):
        qc, kc, vc, gc = inp
        b = jnp.cumsum(gc, axis=1)
        o_inter = jnp.einsum('bthk,bhkv->bthv', qc * jnp.exp(b), state)
        diff = b[:, :, None] - b[:, None, :]
        decay = jnp.exp(jnp.where(tri[None, :, :, None, None], diff, -jnp.inf))
        a = jnp.einsum('bthk,bshk,btshk->bhts', qc, kc, decay)
        o_intra = jnp.einsum('bhts,bshv->bthv', a, vc)
        b_last = b[:, -1]
        new_state = jnp.exp(b_last)[..., None] * state + jnp.einsum(
            'bshk,bshv->bhkv', kc * jnp.exp(b_last[:, None] - b), vc)
        return new_state, o_inter + o_intra

    s0 = jnp.zeros((B, H, K, V), jnp.float32)
    _, o = lax.scan(step, s0, (chunks(q), chunks(k), chunks(v), chunks(logf)))
    return jnp.moveaxis(o, 0, 1).reshape(B, T, H, V)


def hgrn2_mixer(hq, hf, hi, hg, lb_param, out_gain, layer):
    B, T = hq.shape[0], hq.shape[1]
    lb = jnp.cumsum(jax.nn.softmax(lb_param.astype(jnp.float32), axis=0), axis=0)[layer]
    f = lb + (1.0 - lb) * jax.nn.sigmoid(hf.astype(jnp.float32))
    logf = jnp.log(f).reshape(B, T, HG_HEADS, HG_KDIM)
    k = (1.0 - f).reshape(B, T, HG_HEADS, HG_KDIM)
    q = jax.nn.silu(hq.astype(jnp.float32)).reshape(B, T, HG_HEADS, HG_KDIM)
    v = hi.astype(jnp.float32).reshape(B, T, HG_HEADS, HG_VDIM)
    o = hgrn2_chunkwise(q, k, v, logf)
    o = rms_norm(o, out_gain.reshape(HG_HEADS, HG_VDIM))
    o = o.reshape(B, T, HG_WIDTH) * jax.nn.silu(hg.astype(jnp.float32))
    return o.astype(hq.dtype)


def setup_inputs(seed: int = 0) -> dict:
    key = jax.random.key(seed)
    ks = jax.random.split(key, 13)
    f32 = jnp.float32

    def w(k, shape, fan_in):
        return jax.random.normal(k, shape, f32) * (fan_in ** -0.5)

    def gain(k, shape):
        return 1.0 + 0.02 * jax.random.normal(k, shape, f32)

    return {
        "x": jax.random.normal(ks[0], (BATCH, SEQ, D_MODEL), f32),
        "norm_pre_mix": gain(ks[1], (DEPTH, D_MODEL)),
        "w_in": w(ks[2], (DEPTH, D_MODEL, IN_WIDTH), D_MODEL),
        "w_out": w(ks[3], (DEPTH, MIX_WIDTH, D_MODEL), MIX_WIDTH),
        "norm_post_mix": gain(ks[4], (DEPTH, D_MODEL)),
        "hg_lower_bound": jax.random.normal(ks[5], (DEPTH + 1, HG_FWIDTH), f32),
        "hg_out_norm": gain(ks[6], (DEPTH, HG_WIDTH)),
        "norm_pre_ffn": gain(ks[7], (DEPTH, D_MODEL)),
        "w_gate": w(ks[8], (DEPTH, D_MODEL, D_FF), D_MODEL),
        "w_up": w(ks[9], (DEPTH, D_MODEL, D_FF), D_MODEL),
        "w_down": w(ks[10], (DEPTH, D_FF, D_MODEL), D_FF),
        "norm_post_ffn": gain(ks[11], (DEPTH, D_MODEL)),
    }


def reference(x, norm_pre_mix, w_in, w_out, norm_post_mix, hg_lower_bound, hg_out_norm,
              norm_pre_ffn, w_gate, w_up, w_down, norm_post_ffn):
    B, T, _ = x.shape
    pos = jnp.arange(T)
    idx_scale = (IDX_HEADS ** -0.5) * (IDX_DIM ** -0.5)
    for l in range(DEPTH):
        h = rms_norm(x, norm_pre_mix[l])
        proj = jnp.einsum('btd,de->bte', h, w_in[l])
        q, k, v, qi, ki, wi, hq, hf, hi, hg = split_columns(proj)
        q = partial_rope(q.reshape(B, T, ATT_HEADS, HEAD_DIM), pos)
        k = partial_rope(k.reshape(B, T, ATT_KV_HEADS, HEAD_DIM), pos)
        v = v.reshape(B, T, ATT_KV_HEADS, HEAD_DIM)
        qi = partial_rope(qi.reshape(B, T, IDX_HEADS, IDX_DIM), pos)
        ki = partial_rope(ki, pos)
        wi = wi * idx_scale
        att_out = dsa_attention(q, k, v, qi, ki, wi)
        hg_out = hgrn2_mixer(hq, hf, hi, hg, hg_lower_bound, hg_out_norm[l], l)
        mix = jnp.einsum('bte,ed->btd', jnp.concatenate([att_out, hg_out], axis=-1), w_out[l])
        x = x + rms_norm(mix, norm_post_mix[l])
        h2 = rms_norm(x, norm_pre_ffn[l])
        ff = jax.nn.silu(jnp.einsum('btd,df->btf', h2, w_gate[l])) * jnp.einsum('btd,df->btf', h2, w_up[l])
        ff = jnp.einsum('btf,fd->btd', ff, w_down[l])
        x = x + rms_norm(ff, norm_post_ffn[l])
    return x
```

```python
import functools

import jax
import jax.numpy as jnp
import numpy as np
from jax import lax
from jax.experimental import pallas as pl
from jax.experimental.pallas import tpu as pltpu

F32 = jnp.float32
BF16 = jnp.bfloat16
I32 = jnp.int32

D_MODEL = 1024
ATT_HEADS = 8
ATT_KV_HEADS = 2
HEAD_DIM = 64
ATT_WIDTH = ATT_HEADS * HEAD_DIM
KV_WIDTH = ATT_KV_HEADS * HEAD_DIM
IDX_HEADS = 8
IDX_DIM = 64
TOPK_MAX = 256
ROPE_THETA = 500000.0
ROPE_FRACTION = 4
HG_HEADS = 4
HG_KDIM = 128
HG_VDIM = 128
HG_FWIDTH = HG_HEADS * HG_KDIM
HG_WIDTH = HG_HEADS * HG_VDIM
D_FF = 2816
NORM_EPS = 1e-6
LAYER = 0

LANES = 128
SUBLANES = 8
VMEM_LIMIT = 56 * 1024 * 1024

ROW_TILE = 256
Q_TILE = 256
HG_CHUNK = 128
HG_SUB = 16
FFN_ROW_TILE = 512
FFN_COL_TILE = 1408

ROPE_DIMS = HEAD_DIM // ROPE_FRACTION
ROPE_HALF = ROPE_DIMS // 2

INT_MIN = np.int32(-2 ** 31)
NEG_BIG = -1e30

C_Q = 0
C_K = 512
C_V = 640
C_QI = 768
C_KI = 1280
C_WI = 1408
C_HG = 1536
IN_COLS = C_HG + 4 * HG_WIDTH


def _sigmoid(x):
    return 1.0 / (1.0 + jnp.exp(-x))


def _dot(a, b):
    return jnp.dot(a, b, preferred_element_type=F32)


def _dot_nt(a, b):
    return lax.dot_general(a, b, (((1,), (1,)), ((), ())), preferred_element_type=F32)


def _in_proj_kernel(x_ref, g_ref, w_ref, cos_ref, sa_ref, sb_ref,
                    q_ref, k4t_ref, v4_ref, qit_ref, ki2_ref, wit_ref, hraw_ref, *, idx_scale):
    x = x_ref[...]
    ms = jnp.mean(x * x, axis=-1, keepdims=True)
    h = (x * lax.rsqrt(ms + NORM_EPS) * g_ref[...]).astype(BF16)

    def proj(lo, hi):
        return _dot(h, w_ref[:, lo:hi])

    def rope(r):
        width = r.shape[1]
        reps = width // LANES
        c = jnp.tile(cos_ref[...], (1, reps))
        sa = jnp.tile(sa_ref[...], (1, reps))
        sb = jnp.tile(sb_ref[...], (1, reps))
        return r * c + pltpu.roll(r, width - ROPE_HALF, 1) * sa + pltpu.roll(r, ROPE_HALF, 1) * sb

    def lo_hi_variants(r):
        lane = lax.broadcasted_iota(I32, r.shape, 1)
        low = lane < HEAD_DIM
        sw = pltpu.roll(r, HEAD_DIM, 1)
        zero = jnp.zeros_like(r)
        return (jnp.where(low, r, zero), jnp.where(low, zero, sw),
                jnp.where(low, sw, zero), jnp.where(low, zero, r))

    q_ref[...] = (rope(proj(C_Q, C_K)) * (HEAD_DIM ** -0.5)).astype(BF16)

    kr = rope(proj(C_K, C_V))
    for n, kv in enumerate(lo_hi_variants(kr)):
        k4t_ref[0, 0, n * LANES:(n + 1) * LANES, :] = kv.T.astype(BF16)

    vr = proj(C_V, C_QI)
    for n, vv in enumerate(lo_hi_variants(vr)):
        v4_ref[:, n * LANES:(n + 1) * LANES] = vv.astype(BF16)

    qit_ref[0] = rope(proj(C_QI, C_KI)).T.astype(BF16)

    kir = rope(proj(C_KI, C_WI))
    ki2_ref[:, 0:LANES] = kir.astype(BF16)
    ki2_ref[:, LANES:2 * LANES] = pltpu.roll(kir, IDX_DIM, 1).astype(BF16)

    wr = proj(C_WI, C_HG) * idx_scale
    wit_ref[0] = wr.T[0:IDX_HEADS, :]

    hraw_ref[...] = proj(C_HG, IN_COLS)


def _in_proj(x2d, g, w_pad, cos_t, sa_t, sb_t, batch, seq):
    n = x2d.shape[0]
    tm = ROW_TILE
    nt = seq // tm
    idx_scale = (IDX_HEADS ** -0.5) * (IDX_DIM ** -0.5)
    out_shape = (
        jax.ShapeDtypeStruct((n, ATT_WIDTH), BF16),
        jax.ShapeDtypeStruct((batch, nt, 4 * LANES, tm), BF16),
        jax.ShapeDtypeStruct((n, 4 * LANES), BF16),
        jax.ShapeDtypeStruct((batch, IDX_HEADS * IDX_DIM, seq), BF16),
        jax.ShapeDtypeStruct((n, 2 * LANES), BF16),
        jax.ShapeDtypeStruct((batch, IDX_HEADS, seq), F32),
        jax.ShapeDtypeStruct((n, 4 * HG_WIDTH), F32),
    )
    row = lambda i: (i, 0)
    const = lambda i: (0, 0)
    tab = lambda i: (i % nt, 0)
    in_specs = [
        pl.BlockSpec((tm, D_MODEL), row),
        pl.BlockSpec((1, D_MODEL), const),
        pl.BlockSpec((D_MODEL, IN_COLS), const),
        pl.BlockSpec((tm, LANES), tab),
        pl.BlockSpec((tm, LANES), tab),
        pl.BlockSpec((tm, LANES), tab),
    ]
    out_specs = (
        pl.BlockSpec((tm, ATT_WIDTH), row),
        pl.BlockSpec((1, 1, 4 * LANES, tm), lambda i: (i // nt, i % nt, 0, 0)),
        pl.BlockSpec((tm, 4 * LANES), row),
        pl.BlockSpec((1, IDX_HEADS * IDX_DIM, tm), lambda i: (i // nt, 0, i % nt)),
        pl.BlockSpec((tm, 2 * LANES), row),
        pl.BlockSpec((1, IDX_HEADS, tm), lambda i: (i // nt, 0, i % nt)),
        pl.BlockSpec((tm, 4 * HG_WIDTH), row),
    )
    return pl.pallas_call(
        functools.partial(_in_proj_kernel, idx_scale=idx_scale),
        grid=(n // tm,),
        in_specs=in_specs,
        out_specs=out_specs,
        out_shape=out_shape,
        compiler_params=pltpu.CompilerParams(
            dimension_semantics=("arbitrary",), vmem_limit_bytes=VMEM_LIMIT),
        name="in_proj",
    )(x2d, g, w_pad, cos_t, sa_t, sb_t)


def _dsa_kernel(qit_ref, wit_ref, ki2_ref, q_ref, k4t_ref, v4_ref, o_ref,
                key_scr, bias_scr, m_scr, l_scr, acc_scr, *, topk):
    tq = Q_TILE
    sk = ROW_TILE
    qb = pl.program_id(1)
    nk = ((qb + 1) * tq + sk - 1) // sk
    t_idx = qb * tq + lax.broadcasted_iota(I32, (1, tq), 1)
    row_iota = lax.broadcasted_iota(I32, (sk, tq), 0)

    def index_body(j, carry):
        r0 = pl.multiple_of(j * sk, sk)
        ki_lo = ki2_ref[pl.ds(r0, sk), 0:LANES]
        ki_hi = ki2_ref[pl.ds(r0, sk), LANES:2 * LANES]
        score = jnp.zeros((sk, tq), F32)
        for p in range(IDX_HEADS // 2):
            w = qit_ref[0, p * LANES:(p + 1) * LANES, :]
            score = score + jnp.maximum(_dot(ki_lo, w), 0.0) * wit_ref[0, 2 * p:2 * p + 1, :]
            score = score + jnp.maximum(_dot(ki_hi, w), 0.0) * wit_ref[0, 2 * p + 1:2 * p + 2, :]
        bits = lax.bitcast_convert_type(score, I32)
        key = bits ^ (lax.shift_right_arithmetic(bits, 31) & np.int32(0x7FFFFFFF))
        key = jnp.where(score == 0.0, 0, key)
        key = jnp.where(r0 + row_iota <= t_idx, key, INT_MIN)
        key_scr[j] = key
        return carry

    lax.fori_loop(0, nk, index_body, 0)

    def count(pred):
        def body(j, acc):
            m = pred(key_scr[j], j * sk + row_iota)
            ones = jnp.where(m, 1, 0).astype(I32)
            return acc + jnp.sum(ones.reshape(sk // SUBLANES, SUBLANES, tq), axis=0)
        acc = lax.fori_loop(0, nk, body, jnp.zeros((SUBLANES, tq), I32))
        return jnp.sum(acc, axis=0, keepdims=True)

    def bit_body(i, c):
        trial = c ^ lax.shift_left(np.int32(1), 31 - i)
        cnt = count(lambda key, pos: key >= trial)
        return jnp.where(cnt >= topk, trial, c)

    thr = lax.fori_loop(0, 32, bit_body, jnp.full((1, tq), INT_MIN, I32))

    n_gt = count(lambda key, pos: key > thr)
    n_eq = count(lambda key, pos: key == thr)
    need = topk - n_gt
    tie_cut = (n_eq > need) & (thr != INT_MIN)
    any_cut = jnp.max(jnp.where(tie_cut, 1, 0)) > 0

    def cut_search(_):
        def body(i, lim):
            cand = lim | lax.shift_left(np.int32(1), 11 - i)
            cnt = count(lambda key, pos: (key == thr) & (pos < cand))
            return jnp.where(cnt < need, cand, lim)
        return lax.fori_loop(0, 12, body, jnp.zeros((1, tq), I32))

    lim = lax.cond(any_cut, cut_search, lambda _: jnp.full((1, tq), np.int32(2 ** 30), I32), 0)
    lim = jnp.minimum(jnp.where(tie_cut, lim, np.int32(2 ** 30)), t_idx)

    def bias_body(j, carry):
        key = key_scr[j]
        sel = (key > thr) | ((key == thr) & (j * sk + row_iota <= lim))
        bias_scr[j] = jnp.where(sel, 0.0, NEG_BIG).astype(F32).T
        return carry

    lax.fori_loop(0, nk, bias_body, 0)

    m_scr[...] = jnp.full(m_scr.shape, NEG_BIG, F32)
    l_scr[...] = jnp.zeros(l_scr.shape, F32)
    acc_scr[...] = jnp.zeros(acc_scr.shape, F32)

    def att_body(j, carry):
        r0 = pl.multiple_of(j * sk, sk)
        bias = bias_scr[j]
        for h in range(ATT_HEADS):
            pair = h // 2
            variant = 2 * (h // (ATT_HEADS // ATT_KV_HEADS)) + (h % 2)
            qp = q_ref[:, pair * LANES:(pair + 1) * LANES]
            kt = k4t_ref[0, j, variant * LANES:(variant + 1) * LANES, :]
            s = _dot(qp, kt) + bias
            m_prev = m_scr[h][:, 0:1]
            m_new = jnp.maximum(m_prev, jnp.max(s, axis=-1, keepdims=True))
            alpha = jnp.exp(m_prev - m_new)
            p = jnp.exp(s - m_new)
            l_scr[h] = jnp.broadcast_to(alpha * l_scr[h][:, 0:1] + jnp.sum(p, axis=-1, keepdims=True),
                                        (tq, LANES))
            m_scr[h] = jnp.broadcast_to(m_new, (tq, LANES))
            vt = v4_ref[pl.ds(r0, sk), variant * LANES:(variant + 1) * LANES]
            acc_scr[h] = alpha * acc_scr[h] + _dot(p.astype(BF16), vt)
        return carry

    lax.fori_loop(0, nk, att_body, 0)

    for pair in range(ATT_HEADS // 2):
        o = (acc_scr[2 * pair] / l_scr[2 * pair][:, 0:1]
             + acc_scr[2 * pair + 1] / l_scr[2 * pair + 1][:, 0:1])
        o_ref[:, pair * LANES:(pair + 1) * LANES] = o.astype(BF16)


def _dsa(qit, wit, ki2, q, k4t, v4, batch, seq):
    tq = Q_TILE
    sk = ROW_TILE
    nch = seq // sk
    nq = seq // tq
    topk = min(TOPK_MAX, seq // 4)
    in_specs = [
        pl.BlockSpec((1, IDX_HEADS * IDX_DIM, tq), lambda b, i: (b, 0, i)),
        pl.BlockSpec((1, IDX_HEADS, tq), lambda b, i: (b, 0, i)),
        pl.BlockSpec((seq, 2 * LANES), lambda b, i: (b, 0)),
        pl.BlockSpec((tq, ATT_WIDTH), lambda b, i: (b * nq + i, 0)),
        pl.BlockSpec((1, nch, 4 * LANES, sk), lambda b, i: (b, 0, 0, 0)),
        pl.BlockSpec((seq, 4 * LANES), lambda b, i: (b, 0)),
    ]
    return pl.pallas_call(
        functools.partial(_dsa_kernel, topk=topk),
        grid=(batch, nq),
        in_specs=in_specs,
        out_specs=pl.BlockSpec((tq, ATT_WIDTH), lambda b, i: (b * nq + i, 0)),
        out_shape=jax.ShapeDtypeStruct((batch * seq, ATT_WIDTH), BF16),
        scratch_shapes=[
            pltpu.VMEM((nch, sk, tq), I32),
            pltpu.VMEM((nch, tq, sk), F32),
            pltpu.VMEM((ATT_HEADS, tq, LANES), F32),
            pltpu.VMEM((ATT_HEADS, tq, LANES), F32),
            pltpu.VMEM((ATT_HEADS, tq, LANES), F32),
        ],
        compiler_params=pltpu.CompilerParams(
            dimension_semantics=("arbitrary", "arbitrary"), vmem_limit_bytes=VMEM_LIMIT),
        name="dsa",
    )(qit, wit, ki2, q, k4t, v4)


def _hgrn_kernel(hq_ref, hf_ref, hi_ref, hg_ref, lbp_ref, gain_ref, o_ref, st_scr, *, seq):
    c = HG_CHUNK
    sub = HG_SUB
    nsub = c // sub

    lbp = lbp_ref[...]
    e = jnp.exp(lbp - jnp.max(lbp, axis=0, keepdims=True))
    sm = e / jnp.sum(e, axis=0, keepdims=True)
    lb = jnp.sum(sm[0:LAYER + 1], axis=0, keepdims=True)
    gain = gain_ref[...]

    rr = lax.broadcasted_iota(I32, (c, c), 0)
    cc = lax.broadcasted_iota(I32, (c, c), 1)
    tri = jnp.where(rr >= cc, 1.0, 0.0).astype(BF16)
    row_c = lax.broadcasted_iota(I32, (c, HG_KDIM), 0)
    row_s = lax.broadcasted_iota(I32, (sub, HG_KDIM), 0)
    lane_s = lax.broadcasted_iota(I32, (sub, c), 1)

    st_scr[...] = jnp.zeros(st_scr.shape, F32)

    def chunk_body(ci, carry):
        r0 = pl.multiple_of(ci * c, c)
        hq = hq_ref[pl.ds(r0, c), :]
        hf = hf_ref[pl.ds(r0, c), :]
        vv = hi_ref[pl.ds(r0, c), :]
        hg = hg_ref[pl.ds(r0, c), :]
        f = lb + (1.0 - lb) * _sigmoid(hf)
        logf = jnp.log(f)
        kk = 1.0 - f
        qq = hq * _sigmoid(hq)
        kk_b = kk.astype(BF16)
        vv_b = vv.astype(BF16)

        l1 = logf.astype(BF16)
        r1 = logf - l1.astype(F32)
        l2 = r1.astype(BF16)
        l3 = (r1 - l2.astype(F32)).astype(BF16)
        b = _dot(tri, l1) + _dot(tri, l2) + _dot(tri, l3)

        st = st_scr[...]
        o = _dot_nt((qq * jnp.exp(b)).astype(BF16), st.astype(BF16))

        a_rows = []
        for i in range(nsub):
            lo = i * sub
            bs = b[lo:lo + sub, :]
            qs = qq[lo:lo + sub, :]
            if i == 0:
                a_i = jnp.zeros((sub, c), F32)
            else:
                rho = b[lo - 1:lo, :]
                qt = (qs * jnp.exp(bs - rho)).astype(BF16)
                kt = (kk * jnp.exp(jnp.where(row_c < lo, rho - b, -jnp.inf))).astype(BF16)
                a_i = _dot_nt(qt, kt)
            g_rows = []
            for s in range(sub):
                d = jnp.where(row_s >= s, bs - bs[s:s + 1, :], -jnp.inf)
                g_rows.append(qs * jnp.exp(d))
            g = jnp.concatenate(g_rows, axis=0).astype(BF16)
            r = _dot_nt(g, kk_b)
            for s in range(sub):
                a_i = a_i + jnp.where(lane_s == lo + s, r[s * sub:(s + 1) * sub, :], 0.0)
            a_rows.append(a_i)
        a = jnp.concatenate(a_rows, axis=0).astype(BF16)
        o = o + _dot(a, vv_b)

        y = o * lax.rsqrt(jnp.mean(o * o, axis=-1, keepdims=True) + NORM_EPS) * gain
        o_ref[pl.ds(r0, c), :] = (y * (hg * _sigmoid(hg))).astype(o_ref.dtype)

        bl = b[c - 1:c, :]
        kdec = (kk * jnp.exp(bl - b)).astype(BF16)
        st_scr[...] = st * jnp.exp(bl) + _dot(vv.T.astype(BF16), kdec)
        return carry

    lax.fori_loop(0, seq // c, chunk_body, 0)


def _hgrn(hraw, lb_param, out_gain, batch, seq):
    def col(k):
        return pl.BlockSpec((seq, HG_KDIM), lambda b, h, k=k: (b, k * HG_HEADS + h))
    return pl.pallas_call(
        functools.partial(_hgrn_kernel, seq=seq),
        grid=(batch, HG_HEADS),
        in_specs=[col(0), col(1), col(2), col(3),
                  pl.BlockSpec((lb_param.shape[0], HG_KDIM), lambda b, h: (0, h)),
                  pl.BlockSpec((1, HG_VDIM), lambda b, h: (0, h))],
        out_specs=pl.BlockSpec((seq, HG_VDIM), lambda b, h: (b, h)),
        out_shape=jax.ShapeDtypeStruct((batch * seq, HG_WIDTH), BF16),
        scratch_shapes=[pltpu.VMEM((HG_VDIM, HG_KDIM), F32)],
        compiler_params=pltpu.CompilerParams(
            dimension_semantics=("arbitrary", "arbitrary"), vmem_limit_bytes=VMEM_LIMIT),
        name="hgrn2",
    )(hraw, hraw, hraw, hraw, lb_param, out_gain)


def _out_ffn_kernel(x_ref, att_ref, hgo_ref, wo_ref, g1_ref, g2_ref, wg_ref, wu_ref, wd_ref, g3_ref,
                    o_ref, x1_scr, h2_scr, acc_scr):
    j = pl.program_id(1)

    def rms(v, g):
        return v * lax.rsqrt(jnp.mean(v * v, axis=-1, keepdims=True) + NORM_EPS) * g

    @pl.when(j == 0)
    def _():
        mix = _dot(att_ref[...], wo_ref[0:ATT_WIDTH, :]) + _dot(hgo_ref[...], wo_ref[ATT_WIDTH:, :])
        x1 = x_ref[...] + rms(mix, g1_ref[...])
        x1_scr[...] = x1
        h2_scr[...] = rms(x1, g2_ref[...]).astype(BF16)
        acc_scr[...] = jnp.zeros(acc_scr.shape, F32)

    h2 = h2_scr[...]
    gate = _dot(h2, wg_ref[...])
    up = _dot(h2, wu_ref[...])
    act = (gate * _sigmoid(gate) * up).astype(BF16)
    acc_scr[...] += _dot(act, wd_ref[...])

    @pl.when(j == pl.num_programs(1) - 1)
    def _():
        o_ref[...] = x1_scr[...] + rms(acc_scr[...], g3_ref[...])


def _out_ffn(x2d, att, hgo, wo, g1, g2, wg, wu, wd, g3):
    n = x2d.shape[0]
    tm = FFN_ROW_TILE
    tf = FFN_COL_TILE
    row = lambda i, j: (i, 0)
    const = lambda i, j: (0, 0)
    return pl.pallas_call(
        _out_ffn_kernel,
        grid=(n // tm, D_FF // tf),
        in_specs=[
            pl.BlockSpec((tm, D_MODEL), row),
            pl.BlockSpec((tm, ATT_WIDTH), row),
            pl.BlockSpec((tm, HG_WIDTH), row),
            pl.BlockSpec((D_MODEL, D_MODEL), const),
            pl.BlockSpec((1, D_MODEL), const),
            pl.BlockSpec((1, D_MODEL), const),
            pl.BlockSpec((D_MODEL, tf), lambda i, j: (0, j)),
            pl.BlockSpec((D_MODEL, tf), lambda i, j: (0, j)),
            pl.BlockSpec((tf, D_MODEL), lambda i, j: (j, 0)),
            pl.BlockSpec((1, D_MODEL), const),
        ],
        out_specs=pl.BlockSpec((tm, D_MODEL), row),
        out_shape=jax.ShapeDtypeStruct((n, D_MODEL), F32),
        scratch_shapes=[
            pltpu.VMEM((tm, D_MODEL), F32),
            pltpu.VMEM((tm, D_MODEL), BF16),
            pltpu.VMEM((tm, D_MODEL), F32),
        ],
        compiler_params=pltpu.CompilerParams(
            dimension_semantics=("arbitrary", "arbitrary"), vmem_limit_bytes=VMEM_LIMIT),
        name="out_ffn",
    )(x2d, att, hgo, wo, g1, g2, wg, wu, wd, g3)


def _rope_tables(seq):
    inv = ROPE_THETA ** (-jnp.arange(ROPE_HALF, dtype=F32) / ROPE_HALF)
    ang = jnp.arange(seq).astype(F32)[:, None] * inv[None, :]
    cos = jnp.cos(ang)
    sin = jnp.sin(ang)
    rest = HEAD_DIM - ROPE_DIMS
    ones = jnp.ones((seq, rest), F32)
    zeros = jnp.zeros((seq, rest), F32)
    zh = jnp.zeros((seq, ROPE_HALF), F32)
    c = jnp.concatenate([cos, cos, ones], axis=1)
    sa = jnp.concatenate([-sin, zh, zeros], axis=1)
    sb = jnp.concatenate([zh, sin, zeros], axis=1)
    reps = LANES // HEAD_DIM
    return jnp.tile(c, (1, reps)), jnp.tile(sa, (1, reps)), jnp.tile(sb, (1, reps))


def _layer(x2d, batch, seq, norm_pre_mix, w_in, w_out, norm_post_mix, hg_lower_bound, hg_out_norm,
           norm_pre_ffn, w_gate, w_up, w_down, norm_post_ffn):
    ki_end = C_KI + IDX_DIM
    wi_end = ki_end + IDX_HEADS
    zk = jnp.zeros((D_MODEL, LANES - IDX_DIM), w_in.dtype)
    zw = jnp.zeros((D_MODEL, LANES - IDX_HEADS), w_in.dtype)
    w_pad = jnp.concatenate([w_in[:, :ki_end], zk, w_in[:, ki_end:wi_end], zw, w_in[:, wi_end:]],
                            axis=1).astype(BF16)
    cos_t, sa_t, sb_t = _rope_tables(seq)

    q, k4t, v4, qit, ki2, wit, hraw = _in_proj(
        x2d, norm_pre_mix[None, :], w_pad, cos_t, sa_t, sb_t, batch, seq)
    att = _dsa(qit, wit, ki2, q, k4t, v4, batch, seq)
    hgo = _hgrn(hraw, hg_lower_bound, hg_out_norm[None, :], batch, seq)
    return _out_ffn(x2d, att, hgo, w_out.astype(BF16), norm_post_mix[None, :], norm_pre_ffn[None, :],
                    w_gate.astype(BF16), w_up.astype(BF16), w_down.astype(BF16), norm_post_ffn[None, :])


def kernel(x, norm_pre_mix, w_in, w_out, norm_post_mix, hg_lower_bound, hg_out_norm, norm_pre_ffn,
           w_gate, w_up, w_down, norm_post_ffn):
    batch, seq, _ = x.shape
    x2d = x.reshape(batch * seq, D_MODEL)
    depth = w_in.shape[0]
    assert depth == 1, "the recurrence lower bound is evaluated for a single layer"
    for l in range(depth):
        x2d = _layer(x2d, batch, seq, norm_pre_mix[l], w_in[l], w_out[l], norm_post_mix[l],
                     hg_lower_bound, hg_out_norm[l], norm_pre_ffn[l], w_gate[l], w_up[l], w_down[l],
                     norm_post_ffn[l])
    return x2d.reshape(batch, seq, D_MODEL)
```

```python
import functools

import jax
import jax.numpy as jnp
import numpy as np
from jax import lax
from jax.experimental import pallas as pl
from jax.experimental.pallas import tpu as pltpu

F32 = jnp.float32
BF16 = jnp.bfloat16
I32 = jnp.int32

D_MODEL = 1024
ATT_HEADS = 8
ATT_KV_HEADS = 2
HEAD_DIM = 64
ATT_WIDTH = ATT_HEADS * HEAD_DIM
KV_WIDTH = ATT_KV_HEADS * HEAD_DIM
IDX_HEADS = 8
IDX_DIM = 64
TOPK_MAX = 256
ROPE_THETA = 500000.0
ROPE_FRACTION = 4
HG_HEADS = 4
HG_KDIM = 128
HG_VDIM = 128
HG_FWIDTH = HG_HEADS * HG_KDIM
HG_WIDTH = HG_HEADS * HG_VDIM
D_FF = 2816
NORM_EPS = 1e-6
LAYER = 0

LANES = 128
SUBLANES = 8
VMEM_LIMIT = 56 * 1024 * 1024

ROW_TILE = 256
Q_TILE = 256
HG_CHUNK = 128
HG_SUB = 16
FFN_ROW_TILE = 512
FFN_COL_TILE = 1408

ROPE_DIMS = HEAD_DIM // ROPE_FRACTION
ROPE_HALF = ROPE_DIMS // 2

INT_MIN = np.int32(-2 ** 31)
NEG_BIG = -1e30

C_Q = 0
C_K = 512
C_V = 640
C_QI = 768
C_KI = 1280
C_WI = 1408
C_HG = 1536
IN_COLS = C_HG + 4 * HG_WIDTH


def _sigmoid(x):
    return 1.0 / (1.0 + jnp.exp(-x))


def _dot(a, b):
    return jnp.dot(a, b, preferred_element_type=F32)


def _dot_nt(a, b):
    return lax.dot_general(a, b, (((1,), (1,)), ((), ())), preferred_element_type=F32)


def _in_proj_kernel(x_ref, g_ref, w_ref, cos_ref, sa_ref, sb_ref,
                    qt_ref, k_ref, vt_ref, qit_ref, ki_ref, wit_ref, hraw_ref, *, idx_scale):
    x = x_ref[...]
    ms = jnp.mean(x * x, axis=-1, keepdims=True)
    h = (x * lax.rsqrt(ms + NORM_EPS) * g_ref[...]).astype(BF16)

    def proj(lo, hi):
        return _dot(h, w_ref[:, lo:hi])

    def rope(r):
        width = r.shape[1]
        reps = width // LANES
        c = jnp.tile(cos_ref[...], (1, reps))
        sa = jnp.tile(sa_ref[...], (1, reps))
        sb = jnp.tile(sb_ref[...], (1, reps))
        return r * c + pltpu.roll(r, width - ROPE_HALF, 1) * sa + pltpu.roll(r, ROPE_HALF, 1) * sb

    qt_ref[0] = (rope(proj(C_Q, C_K)) * (HEAD_DIM ** -0.5)).T.astype(BF16)
    k_ref[...] = rope(proj(C_K, C_V)).astype(BF16)
    vt_ref[0, 0] = proj(C_V, C_QI).T.astype(BF16)
    qit_ref[0] = rope(proj(C_QI, C_KI)).T.astype(BF16)
    ki_ref[...] = rope(proj(C_KI, C_WI)).astype(BF16)

    wr = proj(C_WI, C_HG) * idx_scale
    wit_ref[0] = wr.T[0:IDX_HEADS, :]

    hraw_ref[...] = proj(C_HG, IN_COLS)


def _in_proj(x2d, g, w_pad, cos_t, sa_t, sb_t, batch, seq):
    n = x2d.shape[0]
    tm = ROW_TILE
    nt = seq // tm
    idx_scale = (IDX_HEADS ** -0.5) * (IDX_DIM ** -0.5)
    out_shape = (
        jax.ShapeDtypeStruct((batch, ATT_WIDTH, seq), BF16),
        jax.ShapeDtypeStruct((n, KV_WIDTH), BF16),
        jax.ShapeDtypeStruct((batch, nt, KV_WIDTH, tm), BF16),
        jax.ShapeDtypeStruct((batch, IDX_HEADS * IDX_DIM, seq), BF16),
        jax.ShapeDtypeStruct((n, LANES), BF16),
        jax.ShapeDtypeStruct((batch, IDX_HEADS, seq), F32),
        jax.ShapeDtypeStruct((n, 4 * HG_WIDTH), F32),
    )
    row = lambda i: (i, 0)
    const = lambda i: (0, 0)
    tab = lambda i: (i % nt, 0)
    tcol = lambda i: (i // nt, 0, i % nt)
    in_specs = [
        pl.BlockSpec((tm, D_MODEL), row),
        pl.BlockSpec((1, D_MODEL), const),
        pl.BlockSpec((D_MODEL, IN_COLS), const),
        pl.BlockSpec((tm, LANES), tab),
        pl.BlockSpec((tm, LANES), tab),
        pl.BlockSpec((tm, LANES), tab),
    ]
    out_specs = (
        pl.BlockSpec((1, ATT_WIDTH, tm), tcol),
        pl.BlockSpec((tm, KV_WIDTH), row),
        pl.BlockSpec((1, 1, KV_WIDTH, tm), lambda i: (i // nt, i % nt, 0, 0)),
        pl.BlockSpec((1, IDX_HEADS * IDX_DIM, tm), tcol),
        pl.BlockSpec((tm, LANES), row),
        pl.BlockSpec((1, IDX_HEADS, tm), tcol),
        pl.BlockSpec((tm, 4 * HG_WIDTH), row),
    )
    return pl.pallas_call(
        functools.partial(_in_proj_kernel, idx_scale=idx_scale),
        grid=(n // tm,),
        in_specs=in_specs,
        out_specs=out_specs,
        out_shape=out_shape,
        compiler_params=pltpu.CompilerParams(
            dimension_semantics=("arbitrary",), vmem_limit_bytes=VMEM_LIMIT),
        name="in_proj",
    )(x2d, g, w_pad, cos_t, sa_t, sb_t)


def _dsa_kernel(qit_ref, wit_ref, ki_ref, qt_ref, k_ref, vt_ref, o_ref,
                key_scr, bias_scr, qiz_scr, qz_scr, m_scr, l_scr, acc_scr, *, topk):
    tq = Q_TILE
    sk = ROW_TILE
    group = ATT_HEADS // ATT_KV_HEADS
    qb = pl.program_id(1)
    nk = ((qb + 1) * tq + sk - 1) // sk
    t_idx = qb * tq + lax.broadcasted_iota(I32, (1, tq), 1)
    row_iota = lax.broadcasted_iota(I32, (sk, tq), 0)

    zeros_half = jnp.zeros((HEAD_DIM, tq), BF16)
    for h in range(IDX_HEADS):
        qiz_scr[:, h * tq:(h + 1) * tq] = jnp.concatenate(
            [qit_ref[0, h * IDX_DIM:(h + 1) * IDX_DIM, :], zeros_half], axis=0)
    for h in range(ATT_HEADS):
        qh = qt_ref[0, h * HEAD_DIM:(h + 1) * HEAD_DIM, :]
        parts = [qh, zeros_half] if h // group == 0 else [zeros_half, qh]
        qz_scr[:, h * tq:(h + 1) * tq] = jnp.concatenate(parts, axis=0)

    def index_body(j, carry):
        r0 = pl.multiple_of(j * sk, sk)
        ki = ki_ref[pl.ds(r0, sk), :]
        rel = _dot(ki, qiz_scr[...])
        score = jnp.zeros((sk, tq), F32)
        for h in range(IDX_HEADS):
            score = score + jnp.maximum(rel[:, h * tq:(h + 1) * tq], 0.0) * wit_ref[0, h:h + 1, :]
        bits = lax.bitcast_convert_type(score, I32)
        key = bits ^ (lax.shift_right_arithmetic(bits, 31) & np.int32(0x7FFFFFFF))
        key = jnp.where(score == 0.0, 0, key)
        key = jnp.where(r0 + row_iota <= t_idx, key, INT_MIN)
        key_scr[j] = key
        return carry

    lax.fori_loop(0, nk, index_body, 0)

    def count(pred):
        def body(j, acc):
            m = pred(key_scr[j], j * sk + row_iota)
            ones = jnp.where(m, 1, 0).astype(I32)
            return acc + jnp.sum(ones.reshape(sk // SUBLANES, SUBLANES, tq), axis=0)
        acc = lax.fori_loop(0, nk, body, jnp.zeros((SUBLANES, tq), I32))
        return jnp.sum(acc, axis=0, keepdims=True)

    def bit_body(i, c):
        trial = c ^ lax.shift_left(np.int32(1), 31 - i)
        cnt = count(lambda key, pos: key >= trial)
        return jnp.where(cnt >= topk, trial, c)

    thr = lax.fori_loop(0, 32, bit_body, jnp.full((1, tq), INT_MIN, I32))

    n_gt = count(lambda key, pos: key > thr)
    n_eq = count(lambda key, pos: key == thr)
    need = topk - n_gt
    tie_cut = (n_eq > need) & (thr != INT_MIN)
    any_cut = jnp.max(jnp.where(tie_cut, 1, 0)) > 0

    def cut_search(_):
        def body(i, lim):
            cand = lim | lax.shift_left(np.int32(1), 11 - i)
            cnt = count(lambda key, pos: (key == thr) & (pos < cand))
            return jnp.where(cnt < need, cand, lim)
        return lax.fori_loop(0, 12, body, jnp.zeros((1, tq), I32))

    lim = lax.cond(any_cut, cut_search, lambda _: jnp.full((1, tq), np.int32(2 ** 30), I32), 0)
    lim = jnp.minimum(jnp.where(tie_cut, lim, np.int32(2 ** 30)), t_idx)

    def bias_body(j, carry):
        key = key_scr[j]
        sel = (key > thr) | ((key == thr) & (j * sk + row_iota <= lim))
        bias_scr[j] = jnp.where(sel, 0.0, NEG_BIG).astype(F32)
        return carry

    lax.fori_loop(0, nk, bias_body, 0)

    m_scr[...] = jnp.full(m_scr.shape, NEG_BIG, F32)
    l_scr[...] = jnp.zeros(l_scr.shape, F32)
    acc_scr[...] = jnp.zeros(acc_scr.shape, F32)

    def att_body(j, carry):
        r0 = pl.multiple_of(j * sk, sk)
        bias = bias_scr[j]
        kc = k_ref[pl.ds(r0, sk), :]
        scores = _dot(kc, qz_scr[...])
        for h in range(ATT_HEADS):
            g = h // group
            s = scores[:, h * tq:(h + 1) * tq] + bias
            m_prev = m_scr[h]
            m_new = jnp.maximum(m_prev, jnp.max(s, axis=0, keepdims=True))
            alpha = jnp.exp(m_prev - m_new)
            p = jnp.exp(s - m_new)
            l_scr[h] = alpha * l_scr[h] + jnp.sum(p, axis=0, keepdims=True)
            m_scr[h] = m_new
            vt = vt_ref[0, j, g * HEAD_DIM:(g + 1) * HEAD_DIM, :]
            acc_scr[h] = alpha * acc_scr[h] + _dot(vt, p.astype(BF16))
        return carry

    lax.fori_loop(0, nk, att_body, 0)

    for pair in range(ATT_HEADS // 2):
        ot = jnp.concatenate([acc_scr[2 * pair] / l_scr[2 * pair],
                              acc_scr[2 * pair + 1] / l_scr[2 * pair + 1]], axis=0)
        o_ref[:, pair * LANES:(pair + 1) * LANES] = ot.T.astype(BF16)


def _dsa(qit, wit, ki, qt, k, vt, batch, seq):
    tq = Q_TILE
    sk = ROW_TILE
    nch = seq // sk
    nq = seq // tq
    topk = min(TOPK_MAX, seq // 4)
    qcol = lambda b, i: (b, 0, i)
    in_specs = [
        pl.BlockSpec((1, IDX_HEADS * IDX_DIM, tq), qcol),
        pl.BlockSpec((1, IDX_HEADS, tq), qcol),
        pl.BlockSpec((seq, LANES), lambda b, i: (b, 0)),
        pl.BlockSpec((1, ATT_WIDTH, tq), qcol),
        pl.BlockSpec((seq, KV_WIDTH), lambda b, i: (b, 0)),
        pl.BlockSpec((1, nch, KV_WIDTH, sk), lambda b, i: (b, 0, 0, 0)),
    ]
    return pl.pallas_call(
        functools.partial(_dsa_kernel, topk=topk),
        grid=(batch, nq),
        in_specs=in_specs,
        out_specs=pl.BlockSpec((tq, ATT_WIDTH), lambda b, i: (b * nq + i, 0)),
        out_shape=jax.ShapeDtypeStruct((batch * seq, ATT_WIDTH), BF16),
        scratch_shapes=[
            pltpu.VMEM((nch, sk, tq), I32),
            pltpu.VMEM((nch, sk, tq), F32),
            pltpu.VMEM((LANES, IDX_HEADS * tq), BF16),
            pltpu.VMEM((LANES, ATT_HEADS * tq), BF16),
            pltpu.VMEM((ATT_HEADS, 1, tq), F32),
            pltpu.VMEM((ATT_HEADS, 1, tq), F32),
            pltpu.VMEM((ATT_HEADS, HEAD_DIM, tq), F32),
        ],
        compiler_params=pltpu.CompilerParams(
            dimension_semantics=("arbitrary", "arbitrary"), vmem_limit_bytes=VMEM_LIMIT),
        name="dsa",
    )(qit, wit, ki, qt, k, vt)


def _hgrn_kernel(hq_ref, hf_ref, hi_ref, hg_ref, lbp_ref, gain_ref, o_ref, st_scr, *, seq):
    c = HG_CHUNK
    sub = HG_SUB
    nsub = c // sub

    lbp = lbp_ref[...]
    e = jnp.exp(lbp - jnp.max(lbp, axis=0, keepdims=True))
    sm = e / jnp.sum(e, axis=0, keepdims=True)
    lb = jnp.sum(sm[0:LAYER + 1], axis=0, keepdims=True)
    gain = gain_ref[...]

    rr = lax.broadcasted_iota(I32, (c, c), 0)
    cc = lax.broadcasted_iota(I32, (c, c), 1)
    tri = jnp.where(rr >= cc, 1.0, 0.0).astype(BF16)
    row_c = lax.broadcasted_iota(I32, (c, HG_KDIM), 0)
    row_s = lax.broadcasted_iota(I32, (sub, HG_KDIM), 0)
    lane_s = lax.broadcasted_iota(I32, (sub, c), 1)

    st_scr[...] = jnp.zeros(st_scr.shape, F32)

    def chunk_body(ci, carry):
        r0 = pl.multiple_of(ci * c, c)
        hq = hq_ref[pl.ds(r0, c), :]
        hf = hf_ref[pl.ds(r0, c), :]
        vv = hi_ref[pl.ds(r0, c), :]
        hg = hg_ref[pl.ds(r0, c), :]
        f = lb + (1.0 - lb) * _sigmoid(hf)
        logf = jnp.log(f)
        kk = 1.0 - f
        qq = hq * _sigmoid(hq)
        kk_b = kk.astype(BF16)
        vv_b = vv.astype(BF16)

        l1 = logf.astype(BF16)
        r1 = logf - l1.astype(F32)
        l2 = r1.astype(BF16)
        l3 = (r1 - l2.astype(F32)).astype(BF16)
        b = _dot(tri, l1) + _dot(tri, l2) + _dot(tri, l3)

        st = st_scr[...]
        o = _dot_nt((qq * jnp.exp(b)).astype(BF16), st.astype(BF16))

        a_rows = []
        for i in range(nsub):
            lo = i * sub
            bs = b[lo:lo + sub, :]
            qs = qq[lo:lo + sub, :]
            if i == 0:
                a_i = jnp.zeros((sub, c), F32)
            else:
                rho = b[lo - 1:lo, :]
                qt = (qs * jnp.exp(bs - rho)).astype(BF16)
                kt = (kk * jnp.exp(jnp.where(row_c < lo, rho - b, -jnp.inf))).astype(BF16)
                a_i = _dot_nt(qt, kt)
            g_rows = []
            for s in range(sub):
                d = jnp.where(row_s >= s, bs - bs[s:s + 1, :], -jnp.inf)
                g_rows.append(qs * jnp.exp(d))
            g = jnp.concatenate(g_rows, axis=0).astype(BF16)
            r = _dot_nt(g, kk_b)
            for s in range(sub):
                a_i = a_i + jnp.where(lane_s == lo + s, r[s * sub:(s + 1) * sub, :], 0.0)
            a_rows.append(a_i)
        a = jnp.concatenate(a_rows, axis=0).astype(BF16)
        o = o + _dot(a, vv_b)

        y = o * lax.rsqrt(jnp.mean(o * o, axis=-1, keepdims=True) + NORM_EPS) * gain
        o_ref[pl.ds(r0, c), :] = (y * (hg * _sigmoid(hg))).astype(o_ref.dtype)

        bl = b[c - 1:c, :]
        kdec = (kk * jnp.exp(bl - b)).astype(BF16)
        st_scr[...] = st * jnp.exp(bl) + _dot(vv.T.astype(BF16), kdec)
        return carry

    lax.fori_loop(0, seq // c, chunk_body, 0)


def _hgrn(hraw, lb_param, out_gain, batch, seq):
    def col(k):
        return pl.BlockSpec((seq, HG_KDIM), lambda b, h, k=k: (b, k * HG_HEADS + h))
    return pl.pallas_call(
        functools.partial(_hgrn_kernel, seq=seq),
        grid=(batch, HG_HEADS),
        in_specs=[col(0), col(1), col(2), col(3),
                  pl.BlockSpec((lb_param.shape[0], HG_KDIM), lambda b, h: (0, h)),
                  pl.BlockSpec((1, HG_VDIM), lambda b, h: (0, h))],
        out_specs=pl.BlockSpec((seq, HG_VDIM), lambda b, h: (b, h)),
        out_shape=jax.ShapeDtypeStruct((batch * seq, HG_WIDTH), BF16),
        scratch_shapes=[pltpu.VMEM((HG_VDIM, HG_KDIM), F32)],
        compiler_params=pltpu.CompilerParams(
            dimension_semantics=("arbitrary", "arbitrary"), vmem_limit_bytes=VMEM_LIMIT),
        name="hgrn2",
    )(hraw, hraw, hraw, hraw, lb_param, out_gain)


def _out_ffn_kernel(x_ref, att_ref, hgo_ref, wo_ref, g1_ref, g2_ref, wg_ref, wu_ref, wd_ref, g3_ref,
                    o_ref, x1_scr, h2_scr, acc_scr):
    j = pl.program_id(1)

    def rms(v, g):
        return v * lax.rsqrt(jnp.mean(v * v, axis=-1, keepdims=True) + NORM_EPS) * g

    @pl.when(j == 0)
    def _():
        mix = _dot(att_ref[...], wo_ref[0:ATT_WIDTH, :]) + _dot(hgo_ref[...], wo_ref[ATT_WIDTH:, :])
        x1 = x_ref[...] + rms(mix, g1_ref[...])
        x1_scr[...] = x1
        h2_scr[...] = rms(x1, g2_ref[...]).astype(BF16)
        acc_scr[...] = jnp.zeros(acc_scr.shape, F32)

    h2 = h2_scr[...]
    gate = _dot(h2, wg_ref[...])
    up = _dot(h2, wu_ref[...])
    act = (gate * _sigmoid(gate) * up).astype(BF16)
    acc_scr[...] += _dot(act, wd_ref[...])

    @pl.when(j == pl.num_programs(1) - 1)
    def _():
        o_ref[...] = x1_scr[...] + rms(acc_scr[...], g3_ref[...])


def _out_ffn(x2d, att, hgo, wo, g1, g2, wg, wu, wd, g3):
    n = x2d.shape[0]
    tm = FFN_ROW_TILE
    tf = FFN_COL_TILE
    row = lambda i, j: (i, 0)
    const = lambda i, j: (0, 0)
    return pl.pallas_call(
        _out_ffn_kernel,
        grid=(n // tm, D_FF // tf),
        in_specs=[
            pl.BlockSpec((tm, D_MODEL), row),
            pl.BlockSpec((tm, ATT_WIDTH), row),
            pl.BlockSpec((tm, HG_WIDTH), row),
            pl.BlockSpec((D_MODEL, D_MODEL), const),
            pl.BlockSpec((1, D_MODEL), const),
            pl.BlockSpec((1, D_MODEL), const),
            pl.BlockSpec((D_MODEL, tf), lambda i, j: (0, j)),
            pl.BlockSpec((D_MODEL, tf), lambda i, j: (0, j)),
            pl.BlockSpec((tf, D_MODEL), lambda i, j: (j, 0)),
            pl.BlockSpec((1, D_MODEL), const),
        ],
        out_specs=pl.BlockSpec((tm, D_MODEL), row),
        out_shape=jax.ShapeDtypeStruct((n, D_MODEL), F32),
        scratch_shapes=[
            pltpu.VMEM((tm, D_MODEL), F32),
            pltpu.VMEM((tm, D_MODEL), BF16),
            pltpu.VMEM((tm, D_MODEL), F32),
        ],
        compiler_params=pltpu.CompilerParams(
            dimension_semantics=("arbitrary", "arbitrary"), vmem_limit_bytes=VMEM_LIMIT),
        name="out_ffn",
    )(x2d, att, hgo, wo, g1, g2, wg, wu, wd, g3)


def _rope_tables(seq):
    inv = ROPE_THETA ** (-jnp.arange(ROPE_HALF, dtype=F32) / ROPE_HALF)
    ang = jnp.arange(seq).astype(F32)[:, None] * inv[None, :]
    cos = jnp.cos(ang)
    sin = jnp.sin(ang)
    rest = HEAD_DIM - ROPE_DIMS
    ones = jnp.ones((seq, rest), F32)
    zeros = jnp.zeros((seq, rest), F32)
    zh = jnp.zeros((seq, ROPE_HALF), F32)
    c = jnp.concatenate([cos, cos, ones], axis=1)
    sa = jnp.concatenate([-sin, zh, zeros], axis=1)
    sb = jnp.concatenate([zh, sin, zeros], axis=1)
    reps = LANES // HEAD_DIM
    return jnp.tile(c, (1, reps)), jnp.tile(sa, (1, reps)), jnp.tile(sb, (1, reps))


def _layer(x2d, batch, seq, norm_pre_mix, w_in, w_out, norm_post_mix, hg_lower_bound, hg_out_norm,
           norm_pre_ffn, w_gate, w_up, w_down, norm_post_ffn):
    ki_end = C_KI + IDX_DIM
    wi_end = ki_end + IDX_HEADS
    zk = jnp.zeros((D_MODEL, LANES - IDX_DIM), w_in.dtype)
    zw = jnp.zeros((D_MODEL, LANES - IDX_HEADS), w_in.dtype)
    w_pad = jnp.concatenate([w_in[:, :ki_end], zk, w_in[:, ki_end:wi_end], zw, w_in[:, wi_end:]],
                            axis=1).astype(BF16)
    cos_t, sa_t, sb_t = _rope_tables(seq)

    qt, k, vt, qit, ki, wit, hraw = _in_proj(
        x2d, norm_pre_mix[None, :], w_pad, cos_t, sa_t, sb_t, batch, seq)
    att = _dsa(qit, wit, ki, qt, k, vt, batch, seq)
    hgo = _hgrn(hraw, hg_lower_bound, hg_out_norm[None, :], batch, seq)
    return _out_ffn(x2d, att, hgo, w_out.astype(BF16), norm_post_mix[None, :], norm_pre_ffn[None, :],
                    w_gate.astype(BF16), w_up.astype(BF16), w_down.astype(BF16), norm_post_ffn[None, :])


def kernel(x, norm_pre_mix, w_in, w_out, norm_post_mix, hg_lower_bound, hg_out_norm, norm_pre_ffn,
           w_gate, w_up, w_down, norm_post_ffn):
    batch, seq, _ = x.shape
    x2d = x.reshape(batch * seq, D_MODEL)
    depth = w_in.shape[0]
    assert depth == 1, "the recurrence lower bound is evaluated for a single layer"
    for l in range(depth):
        x2d = _layer(x2d, batch, seq, norm_pre_mix[l], w_in[l], w_out[l], norm_post_mix[l],
                     hg_lower_bound, hg_out_norm[l], norm_pre_ffn[l], w_gate[l], w_up[l], w_down[l],
                     norm_post_ffn[l])
    return x2d.reshape(batch, seq, D_MODEL)
```

```python
import functools

import jax
import jax.numpy as jnp
import numpy as np
from jax import lax
from jax.experimental import pallas as pl
from jax.experimental.pallas import tpu as pltpu

F32 = jnp.float32
BF16 = jnp.bfloat16
I32 = jnp.int32

D_MODEL = 1024
ATT_HEADS = 8
ATT_KV_HEADS = 2
HEAD_DIM = 64
ATT_WIDTH = ATT_HEADS * HEAD_DIM
KV_WIDTH = ATT_KV_HEADS * HEAD_DIM
IDX_HEADS = 8
IDX_DIM = 64
TOPK_MAX = 256
ROPE_THETA = 500000.0
ROPE_FRACTION = 4
HG_HEADS = 4
HG_KDIM = 128
HG_VDIM = 128
HG_FWIDTH = HG_HEADS * HG_KDIM
HG_WIDTH = HG_HEADS * HG_VDIM
D_FF = 2816
NORM_EPS = 1e-6
LAYER = 0

LANES = 128
SUBLANES = 8
VMEM_LIMIT = 56 * 1024 * 1024

ROW_TILE = 256
Q_TILE = 256
HG_CHUNK = 128
HG_SUB = 16
HG_HEADS_PER_STEP = 4
FFN_ROW_TILE = 512
FFN_COL_TILE = 1408

ROPE_DIMS = HEAD_DIM // ROPE_FRACTION
ROPE_HALF = ROPE_DIMS // 2

INT_MIN = np.int32(-2 ** 31)
NEG_BIG = -1e30

C_Q = 0
C_K = 512
C_V = 640
C_QI = 768
C_KI = 1280
C_WI = 1408
C_HG = 1536
IN_COLS = C_HG + 4 * HG_WIDTH


def _sigmoid(x):
    return 1.0 / (1.0 + jnp.exp(-x))


def _dot(a, b):
    return jnp.dot(a, b, preferred_element_type=F32)


def _dot_nt(a, b):
    return lax.dot_general(a, b, (((1,), (1,)), ((), ())), preferred_element_type=F32)


def _in_proj_kernel(x_ref, g_ref, w_ref, cos_ref, sa_ref, sb_ref,
                    qt_ref, k_ref, vt_ref, qit_ref, ki_ref, wit_ref, hraw_ref, *, idx_scale):
    x = x_ref[...]
    ms = jnp.mean(x * x, axis=-1, keepdims=True)
    h = (x * lax.rsqrt(ms + NORM_EPS) * g_ref[...]).astype(BF16)

    def proj(lo, hi):
        return _dot(h, w_ref[:, lo:hi])

    def rope(r):
        width = r.shape[1]
        reps = width // LANES
        c = jnp.tile(cos_ref[...], (1, reps))
        sa = jnp.tile(sa_ref[...], (1, reps))
        sb = jnp.tile(sb_ref[...], (1, reps))
        return r * c + pltpu.roll(r, width - ROPE_HALF, 1) * sa + pltpu.roll(r, ROPE_HALF, 1) * sb

    qt_ref[0] = (rope(proj(C_Q, C_K)) * (HEAD_DIM ** -0.5)).T.astype(BF16)
    k_ref[...] = rope(proj(C_K, C_V)).astype(BF16)
    vt_ref[0, 0] = proj(C_V, C_QI).T.astype(BF16)
    qit_ref[0] = rope(proj(C_QI, C_KI)).T.astype(BF16)
    ki_ref[...] = rope(proj(C_KI, C_WI)).astype(BF16)

    wr = proj(C_WI, C_HG) * idx_scale
    wit_ref[0] = wr.T[0:IDX_HEADS, :]

    hraw_ref[...] = proj(C_HG, IN_COLS)


def _in_proj(x2d, g, w_pad, cos_t, sa_t, sb_t, batch, seq):
    n = x2d.shape[0]
    tm = ROW_TILE
    nt = seq // tm
    idx_scale = (IDX_HEADS ** -0.5) * (IDX_DIM ** -0.5)
    out_shape = (
        jax.ShapeDtypeStruct((batch, ATT_WIDTH, seq), BF16),
        jax.ShapeDtypeStruct((n, KV_WIDTH), BF16),
        jax.ShapeDtypeStruct((batch, nt, KV_WIDTH, tm), BF16),
        jax.ShapeDtypeStruct((batch, IDX_HEADS * IDX_DIM, seq), BF16),
        jax.ShapeDtypeStruct((n, LANES), BF16),
        jax.ShapeDtypeStruct((batch, IDX_HEADS, seq), F32),
        jax.ShapeDtypeStruct((n, 4 * HG_WIDTH), F32),
    )
    row = lambda i: (i, 0)
    const = lambda i: (0, 0)
    tab = lambda i: (i % nt, 0)
    tcol = lambda i: (i // nt, 0, i % nt)
    in_specs = [
        pl.BlockSpec((tm, D_MODEL), row),
        pl.BlockSpec((1, D_MODEL), const),
        pl.BlockSpec((D_MODEL, IN_COLS), const),
        pl.BlockSpec((tm, LANES), tab),
        pl.BlockSpec((tm, LANES), tab),
        pl.BlockSpec((tm, LANES), tab),
    ]
    out_specs = (
        pl.BlockSpec((1, ATT_WIDTH, tm), tcol),
        pl.BlockSpec((tm, KV_WIDTH), row),
        pl.BlockSpec((1, 1, KV_WIDTH, tm), lambda i: (i // nt, i % nt, 0, 0)),
        pl.BlockSpec((1, IDX_HEADS * IDX_DIM, tm), tcol),
        pl.BlockSpec((tm, LANES), row),
        pl.BlockSpec((1, IDX_HEADS, tm), tcol),
        pl.BlockSpec((tm, 4 * HG_WIDTH), row),
    )
    return pl.pallas_call(
        functools.partial(_in_proj_kernel, idx_scale=idx_scale),
        grid=(n // tm,),
        in_specs=in_specs,
        out_specs=out_specs,
        out_shape=out_shape,
        compiler_params=pltpu.CompilerParams(
            dimension_semantics=("arbitrary",), vmem_limit_bytes=VMEM_LIMIT),
        name="in_proj",
    )(x2d, g, w_pad, cos_t, sa_t, sb_t)


def _dsa_kernel(qit_ref, wit_ref, ki_ref, qt_ref, k_ref, vt_ref, o_ref,
                key_scr, bias_scr, qiz_scr, qz_scr, m_scr, l_scr, acc_scr, *, topk):
    tq = Q_TILE
    sk = ROW_TILE
    group = ATT_HEADS // ATT_KV_HEADS
    qb = pl.program_id(1)
    nk = ((qb + 1) * tq + sk - 1) // sk
    t_idx = qb * tq + lax.broadcasted_iota(I32, (1, tq), 1)
    row_iota = lax.broadcasted_iota(I32, (sk, tq), 0)

    zeros_half = jnp.zeros((HEAD_DIM, tq), BF16)
    for h in range(IDX_HEADS):
        qiz_scr[:, h * tq:(h + 1) * tq] = jnp.concatenate(
            [qit_ref[0, h * IDX_DIM:(h + 1) * IDX_DIM, :], zeros_half], axis=0)
    for h in range(ATT_HEADS):
        qh = qt_ref[0, h * HEAD_DIM:(h + 1) * HEAD_DIM, :]
        parts = [qh, zeros_half] if h // group == 0 else [zeros_half, qh]
        qz_scr[:, h * tq:(h + 1) * tq] = jnp.concatenate(parts, axis=0)

    def index_body(j, carry):
        r0 = pl.multiple_of(j * sk, sk)
        ki = ki_ref[pl.ds(r0, sk), :]
        rel = _dot(ki, qiz_scr[...])
        score = jnp.zeros((sk, tq), F32)
        for h in range(IDX_HEADS):
            score = score + jnp.maximum(rel[:, h * tq:(h + 1) * tq], 0.0) * wit_ref[0, h:h + 1, :]
        bits = lax.bitcast_convert_type(score, I32)
        key = bits ^ (lax.shift_right_arithmetic(bits, 31) & np.int32(0x7FFFFFFF))
        key = jnp.where(score == 0.0, 0, key)
        key = jnp.where(r0 + row_iota <= t_idx, key, INT_MIN)
        key_scr[j] = key
        return carry

    lax.fori_loop(0, nk, index_body, 0)

    def count_ge(trial):
        def body(j, acc):
            ones = jnp.where(key_scr[j] >= trial, 1, 0).astype(I32)
            return acc + jnp.sum(ones.reshape(sk // SUBLANES, SUBLANES, tq), axis=0)
        acc = lax.fori_loop(0, nk, body, jnp.zeros((SUBLANES, tq), I32))
        return jnp.sum(acc, axis=0, keepdims=True)

    def bit_body(i, carry):
        c, n_ge = carry
        trial = c ^ lax.shift_left(np.int32(1), 31 - i)
        cnt = count_ge(trial)
        ok = cnt >= topk
        return jnp.where(ok, trial, c), jnp.where(ok, cnt, n_ge)

    thr, n_ge = lax.fori_loop(0, 32, bit_body,
                              (jnp.full((1, tq), INT_MIN, I32), jnp.zeros((1, tq), I32)))
    active = thr != INT_MIN
    any_cut = jnp.max(jnp.where(active & (n_ge > topk), 1, 0)) > 0

    def bias_plain(_):
        thr_eff = jnp.where(active, thr, INT_MIN + 1)

        def body(j, carry):
            bias_scr[j] = jnp.where(key_scr[j] >= thr_eff, 0.0, NEG_BIG).astype(F32)
            return carry
        lax.fori_loop(0, nk, body, 0)
        return 0

    def bias_ties(_):
        excess = jnp.where(active, n_ge - topk, np.int32(2 ** 30)).astype(F32)
        rr = lax.broadcasted_iota(I32, (sk, sk), 0)
        cc = lax.broadcasted_iota(I32, (sk, sk), 1)
        later_in_chunk = jnp.where(cc > rr, 1.0, 0.0).astype(BF16)

        def body(i, later):
            j = nk - 1 - i
            key = key_scr[j]
            eq = key == thr
            eqf = jnp.where(eq, 1.0, 0.0).astype(F32)
            after = _dot(later_in_chunk, eqf.astype(BF16)) + later
            keep = (key > thr) | (eq & (after >= excess))
            bias_scr[j] = jnp.where(keep, 0.0, NEG_BIG).astype(F32)
            return after[0:1, :] + eqf[0:1, :]
        lax.fori_loop(0, nk, body, jnp.zeros((1, tq), F32))
        return 0

    lax.cond(any_cut, bias_ties, bias_plain, 0)

    m_scr[...] = jnp.full(m_scr.shape, NEG_BIG, F32)
    l_scr[...] = jnp.zeros(l_scr.shape, F32)
    acc_scr[...] = jnp.zeros(acc_scr.shape, F32)

    def att_body(j, carry):
        r0 = pl.multiple_of(j * sk, sk)
        bias = bias_scr[j]
        kc = k_ref[pl.ds(r0, sk), :]
        scores = _dot(kc, qz_scr[...])
        for h in range(ATT_HEADS):
            g = h // group
            s = scores[:, h * tq:(h + 1) * tq] + bias
            m_prev = m_scr[h]
            m_new = jnp.maximum(m_prev, jnp.max(s, axis=0, keepdims=True))
            alpha = jnp.exp(m_prev - m_new)
            p = jnp.exp(s - m_new)
            l_scr[h] = alpha * l_scr[h] + jnp.sum(p, axis=0, keepdims=True)
            m_scr[h] = m_new
            vt = vt_ref[0, j, g * HEAD_DIM:(g + 1) * HEAD_DIM, :]
            acc_scr[h] = alpha * acc_scr[h] + _dot(vt, p.astype(BF16))
        return carry

    lax.fori_loop(0, nk, att_body, 0)

    for pair in range(ATT_HEADS // 2):
        ot = jnp.concatenate([acc_scr[2 * pair] / l_scr[2 * pair],
                              acc_scr[2 * pair + 1] / l_scr[2 * pair + 1]], axis=0)
        o_ref[:, pair * LANES:(pair + 1) * LANES] = ot.T.astype(BF16)


def _dsa(qit, wit, ki, qt, k, vt, batch, seq):
    tq = Q_TILE
    sk = ROW_TILE
    nch = seq // sk
    nq = seq // tq
    topk = min(TOPK_MAX, seq // 4)
    qcol = lambda b, i: (b, 0, i)
    in_specs = [
        pl.BlockSpec((1, IDX_HEADS * IDX_DIM, tq), qcol),
        pl.BlockSpec((1, IDX_HEADS, tq), qcol),
        pl.BlockSpec((seq, LANES), lambda b, i: (b, 0)),
        pl.BlockSpec((1, ATT_WIDTH, tq), qcol),
        pl.BlockSpec((seq, KV_WIDTH), lambda b, i: (b, 0)),
        pl.BlockSpec((1, nch, KV_WIDTH, sk), lambda b, i: (b, 0, 0, 0)),
    ]
    return pl.pallas_call(
        functools.partial(_dsa_kernel, topk=topk),
        grid=(batch, nq),
        in_specs=in_specs,
        out_specs=pl.BlockSpec((tq, ATT_WIDTH), lambda b, i: (b * nq + i, 0)),
        out_shape=jax.ShapeDtypeStruct((batch * seq, ATT_WIDTH), BF16),
        scratch_shapes=[
            pltpu.VMEM((nch, sk, tq), I32),
            pltpu.VMEM((nch, sk, tq), F32),
            pltpu.VMEM((LANES, IDX_HEADS * tq), BF16),
            pltpu.VMEM((LANES, ATT_HEADS * tq), BF16),
            pltpu.VMEM((ATT_HEADS, 1, tq), F32),
            pltpu.VMEM((ATT_HEADS, 1, tq), F32),
            pltpu.VMEM((ATT_HEADS, HEAD_DIM, tq), F32),
        ],
        compiler_params=pltpu.CompilerParams(
            dimension_semantics=("arbitrary", "arbitrary"), vmem_limit_bytes=VMEM_LIMIT),
        name="dsa",
    )(qit, wit, ki, qt, k, vt)


def _hgrn_kernel(hq_ref, hf_ref, hi_ref, hg_ref, lbp_ref, gain_ref, o_ref, st_scr, *, seq):
    c = HG_CHUNK
    sub = HG_SUB
    nsub = c // sub
    nh = HG_HEADS_PER_STEP

    lbp = lbp_ref[...]
    e = jnp.exp(lbp - jnp.max(lbp, axis=0, keepdims=True))
    sm = e / jnp.sum(e, axis=0, keepdims=True)
    lb_all = jnp.sum(sm[0:LAYER + 1], axis=0, keepdims=True)
    gain_all = gain_ref[...]

    rr = lax.broadcasted_iota(I32, (c, c), 0)
    cc = lax.broadcasted_iota(I32, (c, c), 1)
    tri = jnp.where(rr >= cc, 1.0, 0.0).astype(BF16)
    row_s = lax.broadcasted_iota(I32, (sub, HG_KDIM), 0)
    row_h = lax.broadcasted_iota(I32, (SUBLANES, HG_KDIM), 0)
    lane_h = lax.broadcasted_iota(I32, (SUBLANES, c), 1)

    st_scr[...] = jnp.zeros(st_scr.shape, F32)

    def head_chunk(qq, kk, b, vv, st):
        kk_b = kk.astype(BF16)
        vv_b = vv.astype(BF16)
        o = _dot_nt((qq * jnp.exp(b)).astype(BF16), st.astype(BF16))

        a_rows = []
        for i in range(nsub):
            lo = i * sub
            bs = b[lo:lo + sub, :]
            qs = qq[lo:lo + sub, :]
            if i == 0:
                a_i = jnp.zeros((sub, c), F32)
            else:
                rho = b[lo - 1:lo, :]
                qt = (qs * jnp.exp(bs - rho)).astype(BF16)
                kt = jnp.concatenate([(kk[0:lo, :] * jnp.exp(rho - b[0:lo, :])).astype(BF16),
                                      jnp.zeros((c - lo, HG_KDIM), BF16)], axis=0)
                a_i = _dot_nt(qt, kt)
            g_rows = []
            for s in range(sub):
                if s < SUBLANES:
                    d = jnp.where(row_s >= s, bs - bs[s:s + 1, :], -jnp.inf)
                    g_rows.append(qs * jnp.exp(d))
                else:
                    d = jnp.where(row_h >= s - SUBLANES, bs[SUBLANES:, :] - bs[s:s + 1, :], -jnp.inf)
                    g_rows.append(qs[SUBLANES:, :] * jnp.exp(d))
            g = jnp.concatenate(g_rows, axis=0).astype(BF16)
            r = _dot_nt(g, kk_b)
            a_top = a_i[0:SUBLANES, :]
            a_bot = a_i[SUBLANES:, :]
            for s in range(sub):
                pick = lane_h == lo + s
                if s < SUBLANES:
                    a_top = a_top + jnp.where(pick, r[s * sub:s * sub + SUBLANES, :], 0.0)
                    a_bot = a_bot + jnp.where(pick, r[s * sub + SUBLANES:(s + 1) * sub, :], 0.0)
                else:
                    base = SUBLANES * sub + (s - SUBLANES) * SUBLANES
                    a_bot = a_bot + jnp.where(pick, r[base:base + SUBLANES, :], 0.0)
            a_rows.append(a_top)
            a_rows.append(a_bot)
        a = jnp.concatenate(a_rows, axis=0).astype(BF16)
        o = o + _dot(a, vv_b)

        bl = b[c - 1:c, :]
        kdec = (kk * jnp.exp(bl - b)).astype(BF16)
        return o, st * jnp.exp(bl) + _dot(vv.T.astype(BF16), kdec)

    def chunk_body(ci, carry):
        r0 = pl.multiple_of(ci * c, c)
        rows = pl.ds(r0, c)
        hq = hq_ref[rows, :]
        f = lb_all + (1.0 - lb_all) * _sigmoid(hf_ref[rows, :])
        logf = jnp.log(f)
        kk = 1.0 - f
        qq = hq * _sigmoid(hq)
        l1 = logf.astype(BF16)
        r1 = logf - l1.astype(F32)
        l2 = r1.astype(BF16)
        l3 = (r1 - l2.astype(F32)).astype(BF16)
        b = _dot(tri, l1) + _dot(tri, l2) + _dot(tri, l3)

        outs = []
        for n in range(nh):
            cols = slice(n * HG_KDIM, (n + 1) * HG_KDIM)
            o, st_new = head_chunk(qq[:, cols], kk[:, cols], b[:, cols], hi_ref[rows, cols], st_scr[n])
            st_scr[n] = st_new
            outs.append(o * lax.rsqrt(jnp.mean(o * o, axis=-1, keepdims=True) + NORM_EPS))
        hg = hg_ref[rows, :]
        y = jnp.concatenate(outs, axis=1) * gain_all
        o_ref[rows, :] = (y * (hg * _sigmoid(hg))).astype(o_ref.dtype)
        return carry

    lax.fori_loop(0, seq // c, chunk_body, 0)


def _hgrn(hraw, lb_param, out_gain, batch, seq):
    nh = HG_HEADS_PER_STEP
    groups = HG_HEADS // nh
    width = nh * HG_KDIM

    def col(k):
        return pl.BlockSpec((seq, width), lambda b, h, k=k: (b, k * groups + h))
    return pl.pallas_call(
        functools.partial(_hgrn_kernel, seq=seq),
        grid=(batch, groups),
        in_specs=[col(0), col(1), col(2), col(3),
                  pl.BlockSpec((lb_param.shape[0], width), lambda b, h: (0, h)),
                  pl.BlockSpec((1, width), lambda b, h: (0, h))],
        out_specs=pl.BlockSpec((seq, width), lambda b, h: (b, h)),
        out_shape=jax.ShapeDtypeStruct((batch * seq, HG_WIDTH), BF16),
        scratch_shapes=[pltpu.VMEM((nh, HG_VDIM, HG_KDIM), F32)],
        compiler_params=pltpu.CompilerParams(
            dimension_semantics=("arbitrary", "arbitrary"), vmem_limit_bytes=VMEM_LIMIT),
        name="hgrn2",
    )(hraw, hraw, hraw, hraw, lb_param, out_gain)


def _out_ffn_kernel(x_ref, att_ref, hgo_ref, wo_ref, g1_ref, g2_ref, wg_ref, wu_ref, wd_ref, g3_ref,
                    o_ref, x1_scr, h2_scr, acc_scr):
    j = pl.program_id(1)

    def rms(v, g):
        return v * lax.rsqrt(jnp.mean(v * v, axis=-1, keepdims=True) + NORM_EPS) * g

    @pl.when(j == 0)
    def _():
        mix = _dot(att_ref[...], wo_ref[0:ATT_WIDTH, :]) + _dot(hgo_ref[...], wo_ref[ATT_WIDTH:, :])
        x1 = x_ref[...] + rms(mix, g1_ref[...])
        x1_scr[...] = x1
        h2_scr[...] = rms(x1, g2_ref[...]).astype(BF16)
        acc_scr[...] = jnp.zeros(acc_scr.shape, F32)

    h2 = h2_scr[...]
    gate = _dot(h2, wg_ref[...])
    up = _dot(h2, wu_ref[...])
    act = (gate * _sigmoid(gate) * up).astype(BF16)
    acc_scr[...] += _dot(act, wd_ref[...])

    @pl.when(j == pl.num_programs(1) - 1)
    def _():
        o_ref[...] = x1_scr[...] + rms(acc_scr[...], g3_ref[...])


def _out_ffn(x2d, att, hgo, wo, g1, g2, wg, wu, wd, g3):
    n = x2d.shape[0]
    tm = FFN_ROW_TILE
    tf = FFN_COL_TILE
    row = lambda i, j: (i, 0)
    const = lambda i, j: (0, 0)
    return pl.pallas_call(
        _out_ffn_kernel,
        grid=(n // tm, D_FF // tf),
        in_specs=[
            pl.BlockSpec((tm, D_MODEL), row),
            pl.BlockSpec((tm, ATT_WIDTH), row),
            pl.BlockSpec((tm, HG_WIDTH), row),
            pl.BlockSpec((D_MODEL, D_MODEL), const),
            pl.BlockSpec((1, D_MODEL), const),
            pl.BlockSpec((1, D_MODEL), const),
            pl.BlockSpec((D_MODEL, tf), lambda i, j: (0, j)),
            pl.BlockSpec((D_MODEL, tf), lambda i, j: (0, j)),
            pl.BlockSpec((tf, D_MODEL), lambda i, j: (j, 0)),
            pl.BlockSpec((1, D_MODEL), const),
        ],
        out_specs=pl.BlockSpec((tm, D_MODEL), row),
        out_shape=jax.ShapeDtypeStruct((n, D_MODEL), F32),
        scratch_shapes=[
            pltpu.VMEM((tm, D_MODEL), F32),
            pltpu.VMEM((tm, D_MODEL), BF16),
            pltpu.VMEM((tm, D_MODEL), F32),
        ],
        compiler_params=pltpu.CompilerParams(
            dimension_semantics=("arbitrary", "arbitrary"), vmem_limit_bytes=VMEM_LIMIT),
        name="out_ffn",
    )(x2d, att, hgo, wo, g1, g2, wg, wu, wd, g3)


def _rope_tables(seq):
    inv = ROPE_THETA ** (-jnp.arange(ROPE_HALF, dtype=F32) / ROPE_HALF)
    ang = jnp.arange(seq).astype(F32)[:, None] * inv[None, :]
    cos = jnp.cos(ang)
    sin = jnp.sin(ang)
    rest = HEAD_DIM - ROPE_DIMS
    ones = jnp.ones((seq, rest), F32)
    zeros = jnp.zeros((seq, rest), F32)
    zh = jnp.zeros((seq, ROPE_HALF), F32)
    c = jnp.concatenate([cos, cos, ones], axis=1)
    sa = jnp.concatenate([-sin, zh, zeros], axis=1)
    sb = jnp.concatenate([zh, sin, zeros], axis=1)
    reps = LANES // HEAD_DIM
    return jnp.tile(c, (1, reps)), jnp.tile(sa, (1, reps)), jnp.tile(sb, (1, reps))


def _layer(x2d, batch, seq, norm_pre_mix, w_in, w_out, norm_post_mix, hg_lower_bound, hg_out_norm,
           norm_pre_ffn, w_gate, w_up, w_down, norm_post_ffn):
    ki_end = C_KI + IDX_DIM
    wi_end = ki_end + IDX_HEADS
    zk = jnp.zeros((D_MODEL, LANES - IDX_DIM), w_in.dtype)
    zw = jnp.zeros((D_MODEL, LANES - IDX_HEADS), w_in.dtype)
    w_pad = jnp.concatenate([w_in[:, :ki_end], zk, w_in[:, ki_end:wi_end], zw, w_in[:, wi_end:]],
                            axis=1).astype(BF16)
    cos_t, sa_t, sb_t = _rope_tables(seq)

    qt, k, vt, qit, ki, wit, hraw = _in_proj(
        x2d, norm_pre_mix[None, :], w_pad, cos_t, sa_t, sb_t, batch, seq)
    att = _dsa(qit, wit, ki, qt, k, vt, batch, seq)
    hgo = _hgrn(hraw, hg_lower_bound, hg_out_norm[None, :], batch, seq)
    return _out_ffn(x2d, att, hgo, w_out.astype(BF16), norm_post_mix[None, :], norm_pre_ffn[None, :],
                    w_gate.astype(BF16), w_up.astype(BF16), w_down.astype(BF16), norm_post_ffn[None, :])


def kernel(x, norm_pre_mix, w_in, w_out, norm_post_mix, hg_lower_bound, hg_out_norm, norm_pre_ffn,
           w_gate, w_up, w_down, norm_post_ffn):
    batch, seq, _ = x.shape
    x2d = x.reshape(batch * seq, D_MODEL)
    depth = w_in.shape[0]
    assert depth == 1, "the recurrence lower bound is evaluated for a single layer"
    for l in range(depth):
        x2d = _layer(x2d, batch, seq, norm_pre_mix[l], w_in[l], w_out[l], norm_post_mix[l],
                     hg_lower_bound, hg_out_norm[l], norm_pre_ffn[l], w_gate[l], w_up[l], w_down[l],
                     norm_post_ffn[l])
    return x2d.reshape(batch, seq, D_MODEL)
```
